```python
import math
import jax, jax.numpy as jnp
from jax import lax
import numpy as np

D_MODEL = 2048
BATCH = 4
SEQ = 2048
DEPTH = 1
DEC_BATCH = 32
DEC_SEQ = 1
PAST_LEN = 16384
PAGE_SIZE = 128

H_A = 4
DK_A = 128
DV_A = 2 * DK_A
H_B = 4
DK_B = 256
DV_B = 256
CHUNK = 128
N_KEYS = 128
N_EXPERTS = N_KEYS * N_KEYS
PEER_HEADS = 8
PEER_TOPK = 16
PEER_DQ = 256
PEER_BLOCK = 128
ROPE_THETA = 10000.0
EPS = 1e-6
QBLOCK = 128

A_QK = H_A * 2 * DK_A
A_V = H_A * DV_A
B_QK = H_B * DK_B
B_V = H_B * DV_B
IN_SIZES = (A_QK, A_QK, A_V, B_QK, B_QK, B_V, B_V, H_B, H_B)
IN_WIDTH = sum(IN_SIZES)
SPLIT_POINTS = tuple(int(s) for s in np.cumsum(IN_SIZES)[:-1])
MIX_WIDTH = A_V + B_V

kernel_name = 'hymba_diffattn_mlstm_peer_step'


def rmsnorm(x, g):
    xf = x.astype(jnp.float32)
    y = xf * lax.rsqrt(jnp.mean(xf * xf, axis=-1, keepdims=True) + EPS)
    return (y * g.astype(jnp.float32)).astype(x.dtype)


def rope(x, pos):
    d = x.shape[-1]
    half = d // 2
    inv_freq = 1.0 / (ROPE_THETA ** (jnp.arange(half, dtype=jnp.float32) * (2.0 / d)))
    ang = pos.astype(jnp.float32)[:, None] * inv_freq[None, :]
    shape = (1, x.shape[1]) + (1,) * (x.ndim - 3) + (half,)
    cos = jnp.cos(ang).reshape(shape)
    sin = jnp.sin(ang).reshape(shape)
    xf = x.astype(jnp.float32)
    x1, x2 = xf[..., :half], xf[..., half:]
    return jnp.concatenate([x1 * cos - x2 * sin, x2 * cos + x1 * sin], axis=-1).astype(x.dtype)


def diff_lambda(lam_qk, lam_init):
    lf = lam_qk.astype(jnp.float32)
    return jnp.exp(jnp.sum(lf[0] * lf[1])) - jnp.exp(jnp.sum(lf[2] * lf[3])) + lam_init


def mix_inputs(x, pos, g_mix, w_in, b_i, b_f):
    B, T, _ = x.shape
    h = rmsnorm(x, g_mix)
    proj = h @ w_in
    aq, ak, av, bq, bk, bv, bo, bi, bf = jnp.split(proj, SPLIT_POINTS, axis=-1)
    aq = rope(aq.reshape(B, T, H_A, 2, DK_A), pos)
    ak = rope(ak.reshape(B, T, H_A, 2, DK_A), pos)
    av = av.reshape(B, T, H_A, DV_A)
    bq = bq.reshape(B, T, H_B, DK_B)
    bk = bk.reshape(B, T, H_B, DK_B)
    bv = bv.reshape(B, T, H_B, DV_B)
    bo = bo.reshape(B, T, H_B, DV_B)
    return aq, ak, av, bq, bk, bv, bo, bi + b_i, bf + b_f


def diff_attn_prompt(q, k, v, lam):
    B, T = q.shape[:2]
    qb = math.gcd(T, QBLOCK)
    nq = T // qb
    scale = DK_A ** -0.5
    q_blocks = q.reshape(B, nq, qb, H_A, 2, DK_A).swapaxes(0, 1)
    starts = jnp.arange(nq, dtype=jnp.int32) * qb
    kpos = jnp.arange(T, dtype=jnp.int32)

    def block(args):
        qblk, start = args
        s = jnp.einsum('bqhcd,bkhcd->bhcqk', qblk, k).astype(jnp.float32) * scale
        qpos = start + jnp.arange(qb, dtype=jnp.int32)
        mask = kpos[None, :] <= qpos[:, None]
        p = jax.nn.softmax(jnp.where(mask, s, -jnp.inf), axis=-1)
        w = p[:, :, 0] - lam * p[:, :, 1]
        return jnp.einsum('bhqk,bkhd->bqhd', w.astype(v.dtype), v)

    out = lax.map(block, (q_blocks, starts))
    return out.swapaxes(0, 1).reshape(B, T, H_A, DV_A)


def diff_attn_sample(q, k, v, cache_k, cache_v, layer, page_table, lam):
    DB, T = q.shape[:2]
    n_pages = page_table.shape[1]
    psz = cache_k.shape[2]
    past = n_pages * psz
    scale = DK_A ** -0.5
    pt_cols = page_table.T

    def page_scores(pt):
        kp = cache_k[layer, pt]
        return jnp.einsum('bqhcd,bphcd->bhcqp', q, kp).astype(jnp.float32)

    s_past = lax.map(page_scores, pt_cols)
    s_past = s_past.transpose(1, 2, 3, 4, 0, 5).reshape(DB, H_A, 2, T, past)
    s_new = jnp.einsum('bqhcd,bkhcd->bhcqk', q, k).astype(jnp.float32)
    causal = jnp.tril(jnp.ones((T, T), dtype=bool))
    s_new = jnp.where(causal, s_new, -jnp.inf)
    p = jax.nn.softmax(jnp.concatenate([s_past, s_new], axis=-1) * scale, axis=-1)
    w = p[:, :, 0] - lam * p[:, :, 1]
    w_past = w[..., :past].reshape(DB, H_A, T, n_pages, psz).transpose(3, 0, 1, 2, 4)
    acc0 = jnp.einsum('bhqk,bkhd->bqhd', w[..., past:].astype(v.dtype), v)

    def accum(acc, xs):
        pt, wj = xs
        vp = cache_v[layer, pt]
        return acc + jnp.einsum('bhqp,bphd->bqhd', wj.astype(vp.dtype), vp), None

    out, _ = lax.scan(accum, acc0, (pt_cols, w_past))
    return out


def mlstm(q, k, v, i_pre, f_pre, C0, n0, m0):
    B, T = q.shape[:2]
    L = math.gcd(T, CHUNK)
    nc = T // L
    f32 = jnp.float32

    def to_chunks(a):
        return a.reshape((B, nc, L) + a.shape[2:]).swapaxes(0, 1)

    xs = (to_chunks(q.astype(f32)), to_chunks(k.astype(f32) * (DK_B ** -0.5)), to_chunks(v.astype(f32)),
          to_chunks(i_pre.astype(f32)), to_chunks(jax.nn.log_sigmoid(f_pre.astype(f32))))
    causal = jnp.tril(jnp.ones((L, L), dtype=bool))[None, :, :, None]

    def step(carry, inp):
        C, n, m = carry
        qc, kc, vc, ic, fc = inp
        b = jnp.cumsum(fc, axis=1)
        a = ic - b
        m_t = b + jnp.maximum(m[:, None, :], lax.cummax(a, axis=1))
        logD = a[:, None, :, :] + (b - m_t)[:, :, None, :]
        D = jnp.exp(jnp.where(causal, logD, -jnp.inf))
        inter = jnp.exp(b + m[:, None, :] - m_t)
        S = jnp.einsum('bthd,bshd->btsh', qc, kc) * D
        num = jnp.einsum('btsh,bshv->bthv', S, vc) + inter[..., None] * jnp.einsum('bthd,bhdv->bthv', qc, C)
        den = jnp.sum(S, axis=2) + inter * jnp.einsum('bthd,bhd->bth', qc, n)
        h = num / jnp.maximum(jnp.abs(den), jnp.exp(-m_t))[..., None]
        m_end = m_t[:, -1]
        w_end = jnp.exp(a + (b[:, -1] - m_end)[:, None, :])
        f_end = jnp.exp(b[:, -1] + m - m_end)
        C_new = f_end[..., None, None] * C + jnp.einsum('bsh,bshd,bshv->bhdv', w_end, kc, vc)
        n_new = f_end[..., None] * n + jnp.einsum('bsh,bshd->bhd', w_end, kc)
        return (C_new, n_new, m_end), h

    (C1, n1, m1), hs = lax.scan(step, (C0.astype(f32), n0.astype(f32), m0.astype(f32)), xs)
    h = hs.swapaxes(0, 1).reshape(B, T, H_B, DV_B).astype(q.dtype)
    return h, C1, n1, m1


def mix_outputs(x, a_out, b_out, bo, lam_init, g_diff, g_mlstm, w_out):
    B, T, _ = x.shape
    a = rmsnorm(a_out, g_diff) * (1.0 - lam_init)
    b = rmsnorm(b_out, g_mlstm) * jax.nn.sigmoid(bo)
    cat = jnp.concatenate([a.reshape(B, T, A_V), b.reshape(B, T, B_V)], axis=-1)
    return x + cat @ w_out


def peer_ffn(x, g_ffn, w_q, sub_keys, u_emb, v_emb):
    B, T, D = x.shape
    h = rmsnorm(x, g_ffn).reshape(B * T, D)
    ntok = B * T
    q = (h @ w_q).reshape(ntok, PEER_HEADS, 2, PEER_DQ // 2)
    s = jnp.einsum('nhcd,hckd->nhck', q, sub_keys).astype(jnp.float32)
    s_top, i_top = lax.top_k(s, PEER_TOPK)
    cand = (s_top[:, :, 0, :, None] + s_top[:, :, 1, None, :]).reshape(ntok, PEER_HEADS, PEER_TOPK * PEER_TOPK)
    cand_idx = (i_top[:, :, 0, :, None] * N_KEYS + i_top[:, :, 1, None, :]).reshape(ntok, PEER_HEADS, PEER_TOPK * PEER_TOPK)
    fs, fi = lax.top_k(cand, PEER_TOPK)
    eidx = jnp.take_along_axis(cand_idx, fi, axis=-1)
    g = jax.nn.softmax(fs, axis=-1).astype(x.dtype)
    blk = math.gcd(ntok, PEER_BLOCK)
    nb = ntok // blk

    def block(args):
        xb, eb, gb = args
        hb = jax.nn.gelu(jnp.einsum('nd,nhkd->nhk', xb, u_emb[eb]))
        return jnp.einsum('nhk,nhkd->nd', gb * hb, v_emb[eb])

    out = lax.map(block, (h.reshape(nb, blk, D), eidx.reshape(nb, blk, PEER_HEADS, PEER_TOPK),
                          g.reshape(nb, blk, PEER_HEADS, PEER_TOPK)))
    return x + out.reshape(B, T, D)


def setup_inputs(seed: int = 0) -> dict:
    key = jax.random.key(seed)
    ks = jax.random.split(key, 24)
    n_pages = PAST_LEN // PAGE_SIZE
    n_pool = (DEC_BATCH * n_pages * 5) // 4
    f32 = jnp.float32

    def nrm(k, shape, s=1.0):
        return jax.random.normal(k, shape, f32) * s

    x_prompt = nrm(ks[0], (BATCH, SEQ, D_MODEL))
    x_sample = nrm(ks[1], (DEC_BATCH, DEC_SEQ, D_MODEL))
    cache_k = nrm(ks[2], (DEPTH, n_pool, PAGE_SIZE, H_A, 2, DK_A))
    cache_v = nrm(ks[3], (DEPTH, n_pool, PAGE_SIZE, H_A, DV_A))
    state_C = nrm(ks[4], (DEPTH, DEC_BATCH, H_B, DK_B, DV_B), 0.1)
    state_n = nrm(ks[5], (DEPTH, DEC_BATCH, H_B, DK_B), 0.5)
    state_m = nrm(ks[6], (DEPTH, DEC_BATCH, H_B))
    page_table = jax.random.permutation(ks[7], n_pool)[:DEC_BATCH * n_pages].reshape(DEC_BATCH, n_pages).astype(jnp.int32)
    g_mix = 1.0 + nrm(ks[8], (DEPTH, D_MODEL), 0.02)
    w_in = nrm(ks[9], (DEPTH, D_MODEL, IN_WIDTH), D_MODEL ** -0.5)
    b_igate = nrm(ks[10], (DEPTH, H_B), 0.1)
    b_fgate = 3.0 + nrm(ks[11], (DEPTH, H_B), 0.5)
    lam_qk = nrm(ks[12], (DEPTH, 4, DK_A), 0.1)
    g_diff = 1.0 + nrm(ks[13], (DEPTH, DV_A), 0.02)
    g_mlstm = 1.0 + nrm(ks[14], (DEPTH, DV_B), 0.02)
    w_out = nrm(ks[15], (DEPTH, MIX_WIDTH, D_MODEL), MIX_WIDTH ** -0.5)
    g_ffn = 1.0 + nrm(ks[16], (DEPTH, D_MODEL), 0.02)
    w_peer_q = nrm(ks[17], (DEPTH, D_MODEL, PEER_HEADS * PEER_DQ), D_MODEL ** -0.5)
    peer_sub_keys = nrm(ks[18], (DEPTH, PEER_HEADS, 2, N_KEYS, PEER_DQ // 2), (PEER_DQ // 2) ** -0.5)
    peer_u = nrm(ks[19], (DEPTH, N_EXPERTS, D_MODEL), D_MODEL ** -0.5)
    peer_v = nrm(ks[20], (DEPTH, N_EXPERTS, D_MODEL), PEER_HEADS ** -0.5)
    g_final = 1.0 + nrm(ks[21], (D_MODEL,), 0.02)
    return {'x_prompt': x_prompt, 'x_sample': x_sample, 'cache_k': cache_k, 'cache_v': cache_v,
            'state_C': state_C, 'state_n': state_n, 'state_m': state_m, 'page_table': page_table,
            'g_mix': g_mix, 'w_in': w_in, 'b_igate': b_igate, 'b_fgate': b_fgate, 'lam_qk': lam_qk,
            'g_diff': g_diff, 'g_mlstm': g_mlstm, 'w_out': w_out, 'g_ffn': g_ffn, 'w_peer_q': w_peer_q,
            'peer_sub_keys': peer_sub_keys, 'peer_u': peer_u, 'peer_v': peer_v, 'g_final': g_final}


def reference(x_prompt, x_sample, cache_k, cache_v, state_C, state_n, state_m, page_table,
              g_mix, w_in, b_igate, b_fgate, lam_qk, g_diff, g_mlstm, w_out,
              g_ffn, w_peer_q, peer_sub_keys, peer_u, peer_v, g_final):
    B, T, _ = x_prompt.shape
    DB, TS, _ = x_sample.shape
    past = page_table.shape[1] * cache_k.shape[2]
    pos_p = jnp.arange(T, dtype=jnp.int32)
    pos_s = past + jnp.arange(TS, dtype=jnp.int32)
    yp, ys = x_prompt, x_sample
    kp_l, vp_l, Cp_l, np_l, mp_l = [], [], [], [], []
    ks_l, vs_l, Cs_l, ns_l, ms_l = [], [], [], [], []
    for l in range(DEPTH):
        lam_init = 0.8 - 0.6 * math.exp(-0.3 * l)
        lam = diff_lambda(lam_qk[l], lam_init)
        aq, ak, av, bq, bk, bv, bo, bi, bf = mix_inputs(yp, pos_p, g_mix[l], w_in[l], b_igate[l], b_fgate[l])
        a_out = diff_attn_prompt(aq, ak, av, lam)
        C0 = jnp.zeros((B, H_B, DK_B, DV_B), jnp.float32)
        n0 = jnp.zeros((B, H_B, DK_B), jnp.float32)
        m0 = jnp.zeros((B, H_B), jnp.float32)
        b_out, C1, n1, m1 = mlstm(bq, bk, bv, bi, bf, C0, n0, m0)
        yp = mix_outputs(yp, a_out, b_out, bo, lam_init, g_diff[l], g_mlstm[l], w_out[l])
        yp = peer_ffn(yp, g_ffn[l], w_peer_q[l], peer_sub_keys[l], peer_u[l], peer_v[l])
        kp_l.append(ak); vp_l.append(av); Cp_l.append(C1); np_l.append(n1); mp_l.append(m1)
        aq, ak, av, bq, bk, bv, bo, bi, bf = mix_inputs(ys, pos_s, g_mix[l], w_in[l], b_igate[l], b_fgate[l])
        a_out = diff_attn_sample(aq, ak, av, cache_k, cache_v, l, page_table, lam)
        b_out, C1, n1, m1 = mlstm(bq, bk, bv, bi, bf, state_C[l], state_n[l], state_m[l])
        ys = mix_outputs(ys, a_out, b_out, bo, lam_init, g_diff[l], g_mlstm[l], w_out[l])
        ys = peer_ffn(ys, g_ffn[l], w_peer_q[l], peer_sub_keys[l], peer_u[l], peer_v[l])
        ks_l.append(ak); vs_l.append(av); Cs_l.append(C1); ns_l.append(n1); ms_l.append(m1)
    y_prompt = rmsnorm(yp, g_final)
    y_sample = rmsnorm(ys, g_final)
    return (y_prompt, y_sample,
            jnp.stack(kp_l), jnp.stack(vp_l), jnp.stack(Cp_l), jnp.stack(np_l), jnp.stack(mp_l),
            jnp.stack(ks_l), jnp.stack(vs_l), jnp.stack(Cs_l), jnp.stack(ns_l), jnp.stack(ms_l))
```

```python
import functools
import math

import jax
import jax.numpy as jnp
from jax import lax
from jax.experimental import pallas as pl
from jax.experimental.pallas import tpu as pltpu

F32 = jnp.float32
BF16 = jnp.bfloat16
EPS = 1e-6
ROPE_THETA = 10000.0
PEER_TOPK = 16
MLSTM_CHUNK = 128
V7X_LANES = 128
V7X_SUBLANES = 8
V7X_VMEM_LIMIT_BYTES = 56 * 1024 * 1024
NEG_INF = float("-inf")

_NT = (((1,), (1,)), ((), ()))


def _params(*sem):
    return pltpu.CompilerParams(dimension_semantics=sem, vmem_limit_bytes=V7X_VMEM_LIMIT_BYTES)


def _rms(x):
    return x * lax.rsqrt(jnp.mean(x * x, axis=-1, keepdims=True) + EPS)


def _in_proj_body(x_ref, g_ref, w_ref, wg_ref, bg_ref, cos_ref, sin_ref,
                  aq_ref, ak_ref, av_ref, bq_ref, bk_ref, bv_ref, bo_ref, gate_ref,
                  h_scr, *, dk_a, k_scale):
    j = pl.program_id(1)

    @pl.when(j == 0)
    def _():
        hb = (_rms(x_ref[...]) * g_ref[...]).astype(BF16)
        h_scr[...] = hb
        gate_ref[...] = jnp.dot(hb, wg_ref[...], preferred_element_type=F32) + bg_ref[...]

    acc = jnp.dot(h_scr[...], w_ref[...], preferred_element_type=F32)

    def rope(a):
        cosf, sinf = cos_ref[...], sin_ref[...]
        outs = []
        for gi in range(a.shape[1] // dk_a):
            xg = a[:, gi * dk_a:(gi + 1) * dk_a]
            outs.append(xg * cosf + pltpu.roll(xg, dk_a // 2, axis=1) * sinf)
        return jnp.concatenate(outs, axis=1)

    def store(idx, ref, fn):
        @pl.when(j == idx)
        def _():
            ref[...] = fn(acc).astype(ref.dtype)

    store(0, aq_ref, rope)
    store(1, ak_ref, rope)
    store(2, av_ref, lambda a: a)
    store(3, bq_ref, lambda a: a)
    store(4, bk_ref, lambda a: a * k_scale)
    store(5, bv_ref, lambda a: a)
    store(6, bo_ref, lambda a: a)


def _in_proj(x2, g, w_main, w_gate, b_gate, cosf, sinf, *, tm, tab_tiles, seg, dk_a, k_scale, narrow):
    m, d = x2.shape
    nd = BF16 if narrow else F32
    out_dtypes = (nd, F32, F32, nd, nd, nd, F32)
    gw = w_gate.shape[1]
    row = lambda i, j: (i, 0)
    return pl.pallas_call(
        functools.partial(_in_proj_body, dk_a=dk_a, k_scale=k_scale),
        grid=(m // tm, 7),
        in_specs=[
            pl.BlockSpec((tm, d), row),
            pl.BlockSpec((1, d), lambda i, j: (0, 0)),
            pl.BlockSpec((d, seg), lambda i, j: (0, j)),
            pl.BlockSpec((d, gw), lambda i, j: (0, 0)),
            pl.BlockSpec((1, gw), lambda i, j: (0, 0)),
            pl.BlockSpec((tm, dk_a), lambda i, j: (i % tab_tiles, 0)),
            pl.BlockSpec((tm, dk_a), lambda i, j: (i % tab_tiles, 0)),
        ],
        out_specs=[pl.BlockSpec((tm, seg), row)] * 7 + [pl.BlockSpec((tm, gw), row)],
        out_shape=[jax.ShapeDtypeStruct((m, seg), dt) for dt in out_dtypes]
        + [jax.ShapeDtypeStruct((m, gw), F32)],
        scratch_shapes=[pltpu.VMEM((tm, d), BF16)],
        compiler_params=_params("parallel", "arbitrary"),
        name="in_proj",
    )(x2, g, w_main, w_gate, b_gate, cosf, sinf)


def _diff_lambda(lam_ref, lam_init):
    lq = lam_ref[...]
    s1 = jnp.sum(lq[0:1] * lq[1:2], axis=1, keepdims=True)
    s2 = jnp.sum(lq[2:3] * lq[3:4], axis=1, keepdims=True)
    return jnp.exp(s1) - jnp.exp(s2) + lam_init


def _attn_prompt_body(q_ref, k_ref, v_ref, lam_ref, g_ref, o_ref, m_scr, l_scr, acc_scr,
                      *, dk, tq, lam_init):
    qi = pl.program_id(2)
    scale = dk ** -0.5
    q = q_ref[...]
    qs = (q[:, :dk], q[:, dk:])
    m_scr[...] = jnp.full(m_scr.shape, NEG_INF, F32)
    l_scr[...] = jnp.zeros(l_scr.shape, F32)
    acc_scr[...] = jnp.zeros(acc_scr.shape, F32)
    row = lax.broadcasted_iota(jnp.int32, (tq, tq), 0)
    col = lax.broadcasted_iota(jnp.int32, (tq, tq), 1)

    def kv_step(j, masked):
        start = pl.multiple_of(j * tq, tq)
        kblk = k_ref[pl.ds(start, tq), :].astype(BF16)
        vblk = v_ref[pl.ds(start, tq), :].astype(BF16)
        for c in range(2):
            s = lax.dot_general(qs[c], kblk[:, c * dk:(c + 1) * dk], _NT,
                                preferred_element_type=F32) * scale
            if masked:
                s = jnp.where(col <= row, s, NEG_INF)
            m_old = m_scr[c][:, 0:1]
            m_new = jnp.maximum(m_old, jnp.max(s, axis=1, keepdims=True))
            alpha = jnp.exp(m_old - m_new)
            p = jnp.exp(s - m_new)
            l_scr[c] = alpha * l_scr[c] + jnp.sum(p, axis=1, keepdims=True)
            acc_scr[c] = alpha * acc_scr[c] + jnp.dot(p.astype(BF16), vblk, preferred_element_type=F32)
            m_scr[c] = jnp.broadcast_to(m_new, (tq, V7X_LANES))

    def body(j, carry):
        kv_step(j, False)
        return carry

    lax.fori_loop(0, qi, body, 0)
    kv_step(qi, True)

    lam = _diff_lambda(lam_ref, lam_init)
    out = acc_scr[0] / l_scr[0][:, 0:1] - lam * (acc_scr[1] / l_scr[1][:, 0:1])
    o_ref[...] = (_rms(out) * g_ref[...] * (1.0 - lam_init)).astype(o_ref.dtype)


def _attn_prompt(aq, ak, av, lam_qk, g_diff, *, batch, seq, heads, dk, dv, tq, lam_init):
    nq = seq // tq
    return pl.pallas_call(
        functools.partial(_attn_prompt_body, dk=dk, tq=tq, lam_init=lam_init),
        grid=(batch, heads, nq),
        in_specs=[
            pl.BlockSpec((tq, 2 * dk), lambda b, h, i: (b * nq + i, h)),
            pl.BlockSpec((seq, 2 * dk), lambda b, h, i: (b, h)),
            pl.BlockSpec((seq, dv), lambda b, h, i: (b, h)),
            pl.BlockSpec((4, dk), lambda b, h, i: (0, 0)),
            pl.BlockSpec((1, dv), lambda b, h, i: (0, 0)),
        ],
        out_specs=pl.BlockSpec((tq, dv), lambda b, h, i: (b * nq + i, h)),
        out_shape=jax.ShapeDtypeStruct((batch * seq, heads * dv), BF16),
        scratch_shapes=[pltpu.VMEM((2, tq, V7X_LANES), F32), pltpu.VMEM((2, tq, V7X_LANES), F32),
                        pltpu.VMEM((2, tq, dv), F32)],
        compiler_params=_params("parallel", "parallel", "arbitrary"),
        name="attn_prompt",
    )(aq, ak, av, lam_qk, g_diff)


def _scan0(x, op, fill):
    n = x.shape[0]
    rows = lax.broadcasted_iota(jnp.int32, x.shape, 0)
    k = 1
    while k < n:
        x = op(x, jnp.where(rows >= k, pltpu.roll(x, k, axis=0), fill))
        k *= 2
    return x


def _mlstm_prompt_body(q_ref, k_ref, v_ref, o_ref, gate_ref, g_ref,
                       out_ref, c_ref, n_ref, m_ref, *, heads, dk, dv):
    c = pl.program_id(1)
    L = q_ref.shape[0]

    @pl.when(c == 0)
    def _():
        c_ref[...] = jnp.zeros(c_ref.shape, F32)
        n_ref[...] = jnp.zeros(n_ref.shape, F32)
        m_ref[...] = jnp.zeros(m_ref.shape, F32)

    gi = gate_ref[:, 0:V7X_LANES]
    fl = jax.nn.log_sigmoid(gate_ref[:, V7X_LANES:2 * V7X_LANES])
    b = _scan0(fl, jnp.add, 0.0)
    a = gi - b
    m_prev = m_ref[0]
    m_t = b + jnp.maximum(m_prev, _scan0(a, jnp.maximum, NEG_INF))
    b_end, m_end = b[L - 1:L, :], m_t[L - 1:L, :]
    bm = b - m_t
    inter = jnp.exp(b + m_prev - m_t)
    emt = jnp.exp(-m_t)
    w_end = jnp.exp(a + (b_end - m_end))
    f_end = jnp.exp(b_end + m_prev - m_end)
    a_t = jnp.transpose(a)
    causal = (lax.broadcasted_iota(jnp.int32, (L, L), 1) <= lax.broadcasted_iota(jnp.int32, (L, L), 0))

    for h in range(heads):
        qh = q_ref[:, h * dk:(h + 1) * dk]
        kh = k_ref[:, h * dk:(h + 1) * dk]
        vh = v_ref[:, h * dv:(h + 1) * dv]
        col = lambda x: x[:, h:h + 1]
        d_mat = jnp.exp(jnp.where(causal, a_t[h:h + 1, :] + col(bm), NEG_INF))
        s = lax.dot_general(qh, kh, _NT, preferred_element_type=F32) * d_mat
        c_old = c_ref[0, h]
        n_old = n_ref[0, h:h + 1, :]
        num = (jnp.dot(s.astype(BF16), vh, preferred_element_type=F32)
               + col(inter) * jnp.dot(qh, c_old.astype(BF16), preferred_element_type=F32))
        den = (jnp.sum(s, axis=1, keepdims=True)
               + col(inter) * jnp.sum(qh.astype(F32) * n_old, axis=1, keepdims=True))
        hh = num / jnp.maximum(jnp.abs(den), col(emt))
        wk = col(w_end) * kh.astype(F32)
        fe = f_end[:, h:h + 1]
        c_ref[0, h] = fe * c_old + jnp.dot(jnp.transpose(wk).astype(BF16), vh, preferred_element_type=F32)
        n_ref[0, h:h + 1, :] = fe * n_old + jnp.sum(wk, axis=0, keepdims=True)
        gated = _rms(hh) * g_ref[...] * jax.nn.sigmoid(o_ref[:, h * dv:(h + 1) * dv])
        out_ref[:, h * dv:(h + 1) * dv] = gated.astype(out_ref.dtype)
    m_ref[0] = m_end


def _mlstm_prompt(bq, bk, bv, bo, gates, g_mlstm, *, batch, seq, heads, dk, dv):
    L = math.gcd(seq, MLSTM_CHUNK)
    nc = seq // L
    tok = lambda b, c: (b * nc + c, 0)
    gw = gates.shape[1]
    return pl.pallas_call(
        functools.partial(_mlstm_prompt_body, heads=heads, dk=dk, dv=dv),
        grid=(batch, nc),
        in_specs=[
            pl.BlockSpec((L, heads * dk), tok),
            pl.BlockSpec((L, heads * dk), tok),
            pl.BlockSpec((L, heads * dv), tok),
            pl.BlockSpec((L, heads * dv), tok),
            pl.BlockSpec((L, gw), tok),
            pl.BlockSpec((1, dv), lambda b, c: (0, 0)),
        ],
        out_specs=[
            pl.BlockSpec((L, heads * dv), tok),
            pl.BlockSpec((1, heads, dk, dv), lambda b, c: (b, 0, 0, 0)),
            pl.BlockSpec((1, heads, dk), lambda b, c: (b, 0, 0)),
            pl.BlockSpec((1, 1, V7X_LANES), lambda b, c: (b, 0, 0)),
        ],
        out_shape=[
            jax.ShapeDtypeStruct((batch * seq, heads * dv), BF16),
            jax.ShapeDtypeStruct((batch, heads, dk, dv), F32),
            jax.ShapeDtypeStruct((batch, heads, dk), F32),
            jax.ShapeDtypeStruct((batch, 1, V7X_LANES), F32),
        ],
        compiler_params=_params("parallel", "arbitrary"),
        name="mlstm_prompt",
    )(bq, bk, bv, bo, gates, g_mlstm)


def _mlstm_step_body(q_ref, k_ref, v_ref, o_ref, gate_ref, g_ref, c0_ref, n0_ref, m0_ref,
                     out_ref, c_ref, n_ref, m_ref, *, heads, dk, dv):
    rows = 2 * V7X_SUBLANES
    gi = gate_ref[0][:, 0:V7X_LANES]
    fl = jax.nn.log_sigmoid(gate_ref[0][:, V7X_LANES:2 * V7X_LANES])
    m_prev = m0_ref[0]
    a = gi - fl
    m_t = fl + jnp.maximum(m_prev, a)
    dd = jnp.exp(a + fl - m_t)
    inter = jnp.exp(fl + m_prev - m_t)
    emt = jnp.exp(-m_t)
    first = lax.broadcasted_iota(jnp.int32, (rows, 1), 0) == 0
    for h in range(heads):
        qh = q_ref[0][:, h * dk:(h + 1) * dk]
        kh = k_ref[0][:, h * dk:(h + 1) * dk]
        vh = v_ref[0][:, h * dv:(h + 1) * dv]
        sc = lambda x: x[:, h:h + 1]
        c_old = c0_ref[0, h]
        n_old = n0_ref[0, h:h + 1, :]
        s = jnp.sum(qh * kh, axis=1, keepdims=True) * sc(dd)
        q_rows = jnp.broadcast_to(qh, (rows, dk)).astype(BF16)
        qc = jnp.dot(q_rows, c_old.astype(BF16), preferred_element_type=F32)[0:1, :]
        num = s * vh + sc(inter) * qc
        den = s + sc(inter) * jnp.sum(qh * n_old, axis=1, keepdims=True)
        hh = num / jnp.maximum(jnp.abs(den), sc(emt))
        k_rows = jnp.where(first, jnp.broadcast_to(kh, (rows, dk)), 0.0)
        v_rows = jnp.broadcast_to(vh, (rows, dv))
        outer = jnp.dot(jnp.transpose(k_rows).astype(BF16), v_rows.astype(BF16), preferred_element_type=F32)
        c_ref[0, h] = sc(inter) * c_old + sc(dd) * outer
        n_ref[0, h:h + 1, :] = sc(inter) * n_old + sc(dd) * kh
        gated = _rms(hh) * g_ref[...] * jax.nn.sigmoid(o_ref[0][:, h * dv:(h + 1) * dv])
        out_ref[0, :, h * dv:(h + 1) * dv] = gated.astype(out_ref.dtype)
    m_ref[0] = m_t


def _mlstm_step(bq, bk, bv, bo, gates, g_mlstm, c0, n0, m0, *, heads, dk, dv):
    db = bq.shape[0]
    gw = gates.shape[-1]
    vec = lambda w: pl.BlockSpec((1, 1, w), lambda b: (b, 0, 0))
    cspec = pl.BlockSpec((1, heads, dk, dv), lambda b: (b, 0, 0, 0))
    nspec = pl.BlockSpec((1, heads, dk), lambda b: (b, 0, 0))
    return pl.pallas_call(
        functools.partial(_mlstm_step_body, heads=heads, dk=dk, dv=dv),
        grid=(db,),
        in_specs=[vec(heads * dk), vec(heads * dk), vec(heads * dv), vec(heads * dv), vec(gw),
                  pl.BlockSpec((1, dv), lambda b: (0, 0)), cspec, nspec, vec(V7X_LANES)],
        out_specs=[vec(heads * dv), cspec, nspec, vec(V7X_LANES)],
        out_shape=[
            jax.ShapeDtypeStruct((db, 1, heads * dv), F32),
            jax.ShapeDtypeStruct((db, heads, dk, dv), F32),
            jax.ShapeDtypeStruct((db, heads, dk), F32),
            jax.ShapeDtypeStruct((db, 1, V7X_LANES), F32),
        ],
        compiler_params=_params("parallel"),
        name="mlstm_step",
    )(bq, bk, bv, bo, gates, g_mlstm, c0, n0, m0)


def _attn_paged_body(pt_ref, q_ref, kn_ref, vn_ref, lam_ref, g_ref, *rest,
                     heads, dk, dv, psz, pages_per_step, lam_init):
    del pt_ref
    k_refs = rest[:pages_per_step]
    v_refs = rest[pages_per_step:2 * pages_per_step]
    o_ref, m_scr, l_scr, acc_scr = rest[2 * pages_per_step:]
    pg = pl.program_id(1)
    rk = 2 * heads
    halves = dv // V7X_LANES
    rv = halves * heads
    scale = dk ** -0.5
    pad_rows = 2 * V7X_SUBLANES

    @pl.when(pg == 0)
    def _():
        m_scr[...] = jnp.full(m_scr.shape, NEG_INF, F32)
        l_scr[...] = jnp.zeros(l_scr.shape, F32)
        acc_scr[...] = jnp.zeros(acc_scr.shape, F32)

    sel = (lax.broadcasted_iota(jnp.int32, (pad_rows, rk * dk), 1) // dk
           == lax.broadcasted_iota(jnp.int32, (pad_rows, rk * dk), 0)).astype(BF16)
    q_row = q_ref[0]

    def scores(k_rows):
        prod = (k_rows * q_row).astype(BF16)
        return lax.dot_general(sel, prod, _NT, preferred_element_type=F32)[0:rk, :] * scale

    s_list, v_list = [], []
    for u in range(pages_per_step):
        k_cat = jnp.concatenate(
            [k_refs[u][0, 0, pl.ds(r, psz, stride=rk), :] for r in range(rk)], axis=1)
        s_list.append(scores(k_cat))
        v_list.append(jnp.concatenate(
            [v_refs[u][0, 0, pl.ds(hf * heads + h, psz, stride=rv), :]
             for h in range(heads) for hf in range(halves)], axis=1).astype(BF16))
    s_all = jnp.concatenate(s_list, axis=1)
    m_old = m_scr[:, 0:1]
    m_new = jnp.maximum(m_old, jnp.max(s_all, axis=1, keepdims=True))
    alpha = jnp.exp(m_old - m_new)
    p = jnp.exp(s_all - m_new)
    l_new = alpha * l_scr[:, 0:1] + jnp.sum(p, axis=1, keepdims=True)
    pv = jnp.zeros(acc_scr.shape, F32)
    for u in range(pages_per_step):
        pu = jnp.concatenate([p[:, u * psz:(u + 1) * psz],
                              jnp.zeros((pad_rows - rk, psz), F32)], axis=0).astype(BF16)
        pv = pv + jnp.dot(pu, v_list[u], preferred_element_type=F32)[0:rk, :]
    acc_scr[...] = alpha * acc_scr[...] + pv
    m_scr[...] = jnp.broadcast_to(m_new, m_scr.shape)
    l_scr[...] = jnp.broadcast_to(l_new, l_scr.shape)

    @pl.when(pg == pl.num_programs(1) - 1)
    def _():
        kn_rows = jnp.broadcast_to(kn_ref[0], (pad_rows, rk * dk))
        s_self = scores(kn_rows)[:, 0:1]
        m_old = m_scr[:, 0:1]
        m_fin = jnp.maximum(m_old, s_self)
        alpha = jnp.exp(m_old - m_fin)
        e_self = jnp.exp(s_self - m_fin)
        l_fin = alpha * l_scr[:, 0:1] + e_self
        o = (alpha * acc_scr[...] + e_self * vn_ref[0]) / l_fin
        lam = _diff_lambda(lam_ref, lam_init)
        for h in range(heads):
            cols = slice(h * dv, (h + 1) * dv)
            a = o[2 * h:2 * h + 1, cols] - lam * o[2 * h + 1:2 * h + 2, cols]
            o_ref[0, :, cols] = (_rms(a) * g_ref[...] * (1.0 - lam_init)).astype(o_ref.dtype)


def _attn_paged(page_table, aq, ak, av, k_pages, v_pages, lam_qk, g_diff, *, layer, heads, dk, dv, psz,
                lam_init):
    db, n_pages = page_table.shape
    pps = math.gcd(n_pages, 4)
    rk, rv = 2 * heads, (dv // V7X_LANES) * heads
    vec = lambda w: pl.BlockSpec((1, 1, w), lambda b, g, pt: (b, 0, 0))

    def page_spec(rows, u):
        return pl.BlockSpec((1, 1, psz * rows, V7X_LANES),
                            lambda b, g, pt: (layer, pt[b * n_pages + g * pps + u], 0, 0))

    grid_spec = pltpu.PrefetchScalarGridSpec(
        num_scalar_prefetch=1,
        grid=(db, n_pages // pps),
        in_specs=[vec(heads * 2 * dk), vec(heads * 2 * dk), vec(heads * dv),
                  pl.BlockSpec((4, dk), lambda b, g, pt: (0, 0)),
                  pl.BlockSpec((1, dv), lambda b, g, pt: (0, 0))]
        + [page_spec(rk, u) for u in range(pps)] + [page_spec(rv, u) for u in range(pps)],
        out_specs=vec(heads * dv),
        scratch_shapes=[pltpu.VMEM((rk, V7X_LANES), F32), pltpu.VMEM((rk, V7X_LANES), F32),
                        pltpu.VMEM((rk, heads * dv), F32)],
    )
    return pl.pallas_call(
        functools.partial(_attn_paged_body, heads=heads, dk=dk, dv=dv, psz=psz, pages_per_step=pps,
                          lam_init=lam_init),
        grid_spec=grid_spec,
        out_shape=jax.ShapeDtypeStruct((db, 1, heads * dv), F32),
        compiler_params=_params("parallel", "arbitrary"),
        name="attn_paged",
    )(page_table.reshape(-1), aq, ak, av, lam_qk, g_diff, *([k_pages] * pps), *([v_pages] * pps))


def _out_proj_body(a_ref, b_ref, wa_ref, wb_ref, x_ref, o_ref):
    acc = jnp.dot(a_ref[...].astype(BF16), wa_ref[...], preferred_element_type=F32)
    acc = acc + jnp.dot(b_ref[...].astype(BF16), wb_ref[...], preferred_element_type=F32)
    o_ref[...] = x_ref[...] + acc


def _out_proj(a_mix, b_mix, w_a, w_b, x2, *, tm, tn):
    m, d = x2.shape
    return pl.pallas_call(
        _out_proj_body,
        grid=(m // tm, d // tn),
        in_specs=[
            pl.BlockSpec((tm, a_mix.shape[1]), lambda i, j: (i, 0)),
            pl.BlockSpec((tm, b_mix.shape[1]), lambda i, j: (i, 0)),
            pl.BlockSpec((w_a.shape[0], tn), lambda i, j: (0, j)),
            pl.BlockSpec((w_b.shape[0], tn), lambda i, j: (0, j)),
            pl.BlockSpec((tm, tn), lambda i, j: (i, j)),
        ],
        out_specs=pl.BlockSpec((tm, tn), lambda i, j: (i, j)),
        out_shape=jax.ShapeDtypeStruct((m, d), F32),
        compiler_params=_params("parallel", "arbitrary"),
        name="out_proj",
    )(a_mix, b_mix, w_a, w_b, x2)


def _top_values(x, count):
    n = x.shape[0]
    rows = lax.broadcasted_iota(jnp.int32, x.shape, 0)
    out_rows = lax.broadcasted_iota(jnp.int32, (count, x.shape[1]), 0)
    vals = jnp.zeros((count, x.shape[1]), F32)
    for t in range(count):
        m = jnp.max(x, axis=0, keepdims=True)
        first = jnp.min(jnp.where(x == m, rows, n), axis=0, keepdims=True)
        x = jnp.where(rows == first, NEG_INF, x)
        vals = jnp.where(out_rows == t, m, vals)
    return vals


def _candidate_sums(a0, a1):
    k = PEER_TOPK
    half = k // 2
    r = lax.broadcasted_iota(jnp.int32, (half, a0.shape[1]), 0)
    groups = [a0 + a1[0:1]]
    for j in range(1, half):
        groups.append(jnp.where(r < k // (j + 1), a0[0:half] + a1[j:j + 1], NEG_INF))
    groups.append(a0[0:1] + a1[half:k])
    return jnp.concatenate(groups, axis=0)


def _peer_query_body(y_ref, g_ref, wq_ref, keys_ref, hb_ref, s0_ref, e0_ref, s1_ref, e1_ref, tau_ref,
                     s_scr, *, peer_heads, dq_half):
    hb = (_rms(y_ref[...]) * g_ref[...]).astype(BF16)
    hb_ref[...] = hb
    q = jnp.dot(hb, wq_ref[...], preferred_element_type=F32)
    for hc in range(2 * peer_heads):
        qhc = q[:, hc * dq_half:(hc + 1) * dq_half].astype(BF16)
        s_scr[hc] = lax.dot_general(keys_ref[hc], qhc, _NT, preferred_element_type=F32)

    def per_head(hd, carry):
        x0 = s_scr[2 * hd]
        x1 = s_scr[2 * hd + 1]
        a0 = _top_values(x0, PEER_TOPK)
        a1 = _top_values(x1, PEER_TOPK)
        fs = _top_values(_candidate_sums(a0, a1), PEER_TOPK)
        z = jnp.sum(jnp.exp(fs - fs[0:1]), axis=0, keepdims=True)
        s0_ref[hd] = x0
        s1_ref[hd] = x1
        e0_ref[hd] = jnp.exp(x0 - a0[0:1]) / z
        e1_ref[hd] = jnp.exp(x1 - a1[0:1])
        tau_ref[pl.ds(hd, 1), :] = fs[PEER_TOPK - 1:PEER_TOPK]
        return carry

    lax.fori_loop(0, peer_heads, per_head, 0)


def _peer_query(y2, g_ffn, w_q, keys, *, peer_heads, n_keys, dq_half):
    m, d = y2.shape
    tn = V7X_LANES
    stat = jax.ShapeDtypeStruct((peer_heads, n_keys, m), F32)
    stat_spec = pl.BlockSpec((peer_heads, n_keys, tn), lambda i: (0, 0, i))
    return pl.pallas_call(
        functools.partial(_peer_query_body, peer_heads=peer_heads, dq_half=dq_half),
        grid=(m // tn,),
        in_specs=[
            pl.BlockSpec((tn, d), lambda i: (i, 0)),
            pl.BlockSpec((1, d), lambda i: (0, 0)),
            pl.BlockSpec(w_q.shape, lambda i: (0, 0)),
            pl.BlockSpec(keys.shape, lambda i: (0, 0, 0)),
        ],
        out_specs=[pl.BlockSpec((tn, d), lambda i: (i, 0)), stat_spec, stat_spec, stat_spec, stat_spec,
                   pl.BlockSpec((peer_heads, tn), lambda i: (0, i))],
        out_shape=[jax.ShapeDtypeStruct((m, d), BF16), stat, stat, stat, stat,
                   jax.ShapeDtypeStruct((peer_heads, m), F32)],
        scratch_shapes=[pltpu.VMEM((2 * peer_heads, n_keys, tn), F32)],
        compiler_params=_params("parallel"),
        name="peer_query",
    )(y2, g_ffn, w_q, keys)


def _peer_dense_body(hb_ref, u_ref, vt_ref, s0_ref, e0_ref, s1_ref, e1_ref, tau_ref, y_ref, g_ref,
                     o_ref, acc_scr, hid_scr, p_scr, *, peer_heads, n_keys, final_norm):
    e = pl.program_id(1)

    @pl.when(e == 0)
    def _():
        acc_scr[...] = jnp.zeros(acc_scr.shape, F32)

    hid = lax.dot_general(u_ref[...], hb_ref[...], _NT, preferred_element_type=F32)
    hid_scr[...] = jax.nn.gelu(hid)

    def per_key0(ii, carry):
        gate = jnp.zeros((n_keys, hid_scr.shape[1]), F32)
        for hd in range(peer_heads):
            s0 = s0_ref[hd, pl.ds(ii, 1), :]
            e0 = e0_ref[hd, pl.ds(ii, 1), :]
            x = s1_ref[hd] + s0
            gate = gate + jnp.where(x >= tau_ref[hd:hd + 1, :], e1_ref[hd] * e0, 0.0)
        rows = pl.ds(pl.multiple_of(ii * n_keys, n_keys), n_keys)
        p_scr[rows, :] = (gate * hid_scr[rows, :]).astype(BF16)
        return carry

    lax.fori_loop(0, hid_scr.shape[0] // n_keys, per_key0, 0)
    acc_scr[...] += jnp.dot(vt_ref[...], p_scr[...], preferred_element_type=F32)

    @pl.when(e == pl.num_programs(1) - 1)
    def _():
        y = y_ref[...] + jnp.transpose(acc_scr[...])
        if final_norm:
            y = _rms(y) * g_ref[...]
        o_ref[...] = y


def _peer_dense(hb, u_bf, vt_bf, s0, e0, s1, e1, tau, y2, g_final, *, tn, te, peer_heads, n_keys, final_norm):
    m, d = y2.shape
    n_exp = u_bf.shape[0]
    kb = te // n_keys
    tok = lambda i, e: (i, 0)
    once = pl.Buffered(1)
    return pl.pallas_call(
        functools.partial(_peer_dense_body, peer_heads=peer_heads, n_keys=n_keys, final_norm=final_norm),
        grid=(m // tn, n_exp // te),
        in_specs=[
            pl.BlockSpec((tn, d), tok),
            pl.BlockSpec((te, d), lambda i, e: (e, 0)),
            pl.BlockSpec((d, te), lambda i, e: (0, e)),
            pl.BlockSpec((peer_heads, kb, tn), lambda i, e: (0, e, i)),
            pl.BlockSpec((peer_heads, kb, tn), lambda i, e: (0, e, i)),
            pl.BlockSpec((peer_heads, n_keys, tn), lambda i, e: (0, 0, i), pipeline_mode=once),
            pl.BlockSpec((peer_heads, n_keys, tn), lambda i, e: (0, 0, i), pipeline_mode=once),
            pl.BlockSpec((peer_heads, tn), lambda i, e: (0, i)),
            pl.BlockSpec((tn, d), tok, pipeline_mode=once),
            pl.BlockSpec((1, d), lambda i, e: (0, 0)),
        ],
        out_specs=pl.BlockSpec((tn, d), tok),
        out_shape=jax.ShapeDtypeStruct((m, d), F32),
        scratch_shapes=[pltpu.VMEM((d, tn), F32), pltpu.VMEM((te, tn), F32), pltpu.VMEM((te, tn), BF16)],
        compiler_params=_params("parallel", "arbitrary"),
        name="peer_dense",
    )(hb, u_bf, vt_bf, s0, e0, s1, e1, tau, y2, g_final)


def _rope_tables(pos, dk):
    half = dk // 2
    inv_freq = 1.0 / (ROPE_THETA ** (jnp.arange(half, dtype=F32) * (2.0 / dk)))
    ang = pos.astype(F32)[:, None] * inv_freq[None, :]
    cos, sin = jnp.cos(ang), jnp.sin(ang)
    return jnp.concatenate([cos, cos], axis=1), jnp.concatenate([-sin, sin], axis=1)


def _tile(n, pref):
    return pref if n % pref == 0 else n


def _pad_rows(x, rows):
    return jnp.pad(x, ((0, rows - x.shape[0]), (0, 0)))


def kernel(x_prompt, x_sample, cache_k, cache_v, state_C, state_n, state_m, page_table, g_mix, w_in, b_igate,
           b_fgate, lam_qk, g_diff, g_mlstm, w_out, g_ffn, w_peer_q, peer_sub_keys, peer_u, peer_v, g_final):
    B, T, D = x_prompt.shape
    DB, TS, _ = x_sample.shape
    depth, n_pool, psz, h_a, _, dk_a = cache_k.shape
    dv_a = cache_v.shape[-1]
    _, _, h_b, dk_b, dv_b = state_C.shape
    _, peer_heads, _, n_keys, dq_half = peer_sub_keys.shape
    n_exp = peer_u.shape[1]
    a_qk, a_v, b_qk, b_v = h_a * 2 * dk_a, h_a * dv_a, h_b * dk_b, h_b * dv_b
    seg = a_qk
    assert TS == 1, "the decode group holds one new token per sequence"
    assert a_v == seg and b_qk == seg and b_v == seg and dk_a == V7X_LANES and h_b <= V7X_LANES
    assert n_keys == V7X_LANES and n_exp == n_keys * n_keys
    past = page_table.shape[1] * psz
    n_p, n_s = B * T, DB * TS
    ns_pad = -(-n_s // V7X_LANES) * V7X_LANES

    cos_p, sin_p = _rope_tables(jnp.arange(T, dtype=jnp.int32), dk_a)
    pos_s = past + jnp.arange(TS, dtype=jnp.int32)
    cos_s, sin_s = (jnp.tile(t, (DB, 1)) for t in _rope_tables(pos_s, dk_a))

    yp = x_prompt.reshape(n_p, D)
    ys = x_sample.reshape(n_s, D)
    k_pages = cache_k.reshape(depth, n_pool, psz * 2 * h_a, dk_a)
    halves = dv_a // V7X_LANES
    v_pages = (cache_v.reshape(depth, n_pool, psz, h_a, halves, V7X_LANES)
               .transpose(0, 1, 2, 4, 3, 5).reshape(depth, n_pool, psz * halves * h_a, V7X_LANES))

    outs = {k: [] for k in ("kp", "vp", "Cp", "np", "mp", "ks", "vs", "Cs", "ns", "ms")}
    for l in range(depth):
        lam_init = 0.8 - 0.6 * math.exp(-0.3 * l)
        last = l == depth - 1
        w_main = w_in[l, :, :7 * seg].astype(BF16)
        gate_w = jnp.zeros((D, 2 * V7X_LANES), F32)
        gate_w = gate_w.at[:, 0:h_b].set(w_in[l, :, 7 * seg:7 * seg + h_b])
        gate_w = gate_w.at[:, V7X_LANES:V7X_LANES + h_b].set(w_in[l, :, 7 * seg + h_b:]).astype(BF16)
        gate_b = jnp.zeros((1, 2 * V7X_LANES), F32)
        gate_b = gate_b.at[0, 0:h_b].set(b_igate[l]).at[0, V7X_LANES:V7X_LANES + h_b].set(b_fgate[l])
        w_oa = w_out[l, :a_v].astype(BF16)
        w_ob = w_out[l, a_v:].astype(BF16)
        w_q = w_peer_q[l].astype(BF16)
        keys = peer_sub_keys[l].reshape(2 * peer_heads, n_keys, dq_half).astype(BF16)
        u_bf = peer_u[l].astype(BF16)
        vt_bf = peer_v[l].astype(BF16).T
        gm, gd, gl, gf = (g[l][None, :] for g in (g_mix, g_diff, g_mlstm, g_ffn))
        gfin = g_final[None, :]
        proj = functools.partial(_in_proj, seg=seg, dk_a=dk_a, k_scale=dk_b ** -0.5)

        tm = _tile(T, 256)
        aq, ak, av, bq, bk, bv, bo, gates = proj(yp, gm, w_main, gate_w, gate_b, cos_p, sin_p,
                                                 tm=tm, tab_tiles=T // tm, narrow=True)
        a_mix = _attn_prompt(aq, ak, av, lam_qk[l], gd, batch=B, seq=T, heads=h_a, dk=dk_a, dv=dv_a,
                             tq=_tile(T, 256), lam_init=lam_init)
        b_mix, c1, n1, m1 = _mlstm_prompt(bq, bk, bv, bo, gates, gl, batch=B, seq=T, heads=h_b, dk=dk_b, dv=dv_b)
        y1 = _out_proj(a_mix, b_mix, w_oa, w_ob, yp, tm=_tile(n_p, 512), tn=_tile(D, 1024))
        hb, s0, e0, s1, e1, tau = _peer_query(y1, gf, w_q, keys, peer_heads=peer_heads, n_keys=n_keys,
                                              dq_half=dq_half)
        yp = _peer_dense(hb, u_bf, vt_bf, s0, e0, s1, e1, tau, y1, gfin, tn=_tile(n_p, 512),
                         te=_tile(n_exp, 1024), peer_heads=peer_heads, n_keys=n_keys, final_norm=last)
        outs["kp"].append(ak.reshape(B, T, h_a, 2, dk_a))
        outs["vp"].append(av.reshape(B, T, h_a, dv_a))
        outs["Cp"].append(c1)
        outs["np"].append(n1)
        outs["mp"].append(m1[:, 0, :h_b])

        aq, ak, av, bq, bk, bv, bo, gates = proj(ys, gm, w_main, gate_w, gate_b, cos_s, sin_s,
                                                 tm=n_s, tab_tiles=1, narrow=False)
        r3 = lambda x: x.reshape(DB, 1, x.shape[-1])
        a_mix = _attn_paged(page_table, r3(aq), r3(ak), r3(av), k_pages, v_pages, lam_qk[l], gd, layer=l,
                            heads=h_a, dk=dk_a, dv=dv_a, psz=psz, lam_init=lam_init)
        m0 = jnp.pad(state_m[l], ((0, 0), (0, V7X_LANES - h_b)))[:, None, :]
        b_mix, c1, n1, m1 = _mlstm_step(r3(bq), r3(bk), r3(bv), r3(bo), r3(gates), gl, state_C[l], state_n[l],
                                        m0, heads=h_b, dk=dk_b, dv=dv_b)
        y1 = _out_proj(a_mix.reshape(n_s, a_v), b_mix.reshape(n_s, b_v), w_oa, w_ob, ys, tm=n_s,
                       tn=_tile(D, 1024))
        y1p = _pad_rows(y1, ns_pad)
        hb, s0, e0, s1, e1, tau = _peer_query(y1p, gf, w_q, keys, peer_heads=peer_heads, n_keys=n_keys,
                                              dq_half=dq_half)
        ys = _peer_dense(hb, u_bf, vt_bf, s0, e0, s1, e1, tau, y1p, gfin, tn=V7X_LANES,
                         te=_tile(n_exp, 1024), peer_heads=peer_heads, n_keys=n_keys, final_norm=last)[:n_s]
        outs["ks"].append(ak.reshape(DB, TS, h_a, 2, dk_a))
        outs["vs"].append(av.reshape(DB, TS, h_a, dv_a))
        outs["Cs"].append(c1)
        outs["ns"].append(n1)
        outs["ms"].append(m1[:, 0, :h_b])

    stack = lambda k: jnp.stack(outs[k])
    return (yp.reshape(B, T, D), ys.reshape(DB, TS, D),
            stack("kp"), stack("vp"), stack("Cp"), stack("np"), stack("mp"),
            stack("ks"), stack("vs"), stack("Cs"), stack("ns"), stack("ms"))
```
